```python
import jax, jax.numpy as jnp
from jax import lax
import numpy as np

D_MODEL = 1024
BATCH = 2
SEQ = 8192
DEPTH = 2
DEC_BATCH = 32
DEC_SEQ = 8
PAST_LEN = 16384
PAGE_SIZE = 128

PLE_DIM = 256
LRU_WIDTH = D_MODEL
LRU_HEADS = 8
LRU_BLOCK = LRU_WIDTH // LRU_HEADS
CONV_WIDTH = 4
LRU_C = 8.0
N_HEADS = 8
KV_HEADS = 4
HEAD_DIM = D_MODEL // N_HEADS
GROUP = N_HEADS // KV_HEADS
MOBA_BLOCK = 256
MOBA_TOPK = 3
Q_BLOCK = 128
ATTN_SCALE = HEAD_DIM ** -0.5
GMLP_WIDTH = D_MODEL
GMLP_GROUPS = 8
GMLP_GW = GMLP_WIDTH // GMLP_GROUPS
CHUNK = 128
D_FF = 2816
N_BRANCH = 3
ALPHA = (2.0 * DEPTH) ** 0.25
BETA = (8.0 * DEPTH) ** -0.25
LN_EPS = 1e-5
IN_SIZES = (LRU_WIDTH, LRU_WIDTH, N_HEADS * HEAD_DIM, KV_HEADS * HEAD_DIM, KV_HEADS * HEAD_DIM, GMLP_WIDTH, GMLP_WIDTH, N_BRANCH * D_MODEL)
D_IN = sum(IN_SIZES)
SPLIT_POINTS = tuple(sum(IN_SIZES[:j + 1]) for j in range(len(IN_SIZES) - 1))

kernel_name = 'hybrid_rglru_moba_gmlp_decode_step'


def _layer_norm(x, g, b):
    xf = x.astype(jnp.float32)
    mu = xf.mean(-1, keepdims=True)
    xc = xf - mu
    var = jnp.mean(xc * xc, -1, keepdims=True)
    return (xc * lax.rsqrt(var + LN_EPS) * g.astype(jnp.float32) + b.astype(jnp.float32)).astype(x.dtype)


def _swiglu(x, w_gate, w_up, w_down):
    return (jax.nn.silu(x @ w_gate) * (x @ w_up)) @ w_down


def _linear_combine(c1, c2):
    a1, b1 = c1
    a2, b2 = c2
    return a1 * a2, a2 * b1 + b2


def _rglru_branch(xr, gr, conv0, h0, w_conv, b_conv, w_a, b_a, w_x, b_x, lam):
    B, T, W = xr.shape
    xpad = jnp.concatenate([conv0.astype(xr.dtype), xr], axis=1)
    xc = b_conv + sum(xpad[:, j:j + T] * w_conv[j] for j in range(CONV_WIDTH))
    xh = xc.reshape(B, T, LRU_HEADS, LRU_BLOCK)
    r = jax.nn.sigmoid(jnp.einsum('bthi,hij->bthj', xh, w_a).reshape(B, T, W) + b_a)
    ig = jax.nn.sigmoid(jnp.einsum('bthi,hij->bthj', xh, w_x).reshape(B, T, W) + b_x)
    log_a = -LRU_C * r.astype(jnp.float32) * jax.nn.softplus(-lam.astype(jnp.float32))
    a = jnp.exp(log_a)
    bx = jnp.sqrt(-jnp.expm1(2.0 * log_a)) * (ig * xc).astype(jnp.float32)
    bx = bx.at[:, 0].add(a[:, 0] * h0.astype(jnp.float32))
    _, hs = lax.associative_scan(_linear_combine, (a, bx), axis=1)
    y = hs.astype(xr.dtype) * jax.nn.gelu(gr)
    return y, hs[:, -1].astype(h0.dtype), xpad[:, T:]


def _gmlp_branch(u, v, ln_g, ln_b, w_s, b_s):
    B, T, W = v.shape
    u = jax.nn.gelu(u)
    v = _layer_norm(jax.nn.gelu(v), ln_g, ln_b)
    L = min(T, CHUNK)
    vh = v.reshape(B, T // L, L, GMLP_GROUPS, GMLP_GW)
    w = w_s[:, :L, :L] * jnp.tril(jnp.ones((L, L), w_s.dtype))
    mix = jnp.einsum('gts,bcsgd->bctgd', w, vh) + b_s[:, :L].T[:, :, None]
    return u * mix.reshape(B, T, W), v


def _moba_core(q, k_own, v_own, own_mask, bmeans, n_full, n_slots, fetch):
    B, Q = q.shape[:2]
    L = k_own.shape[1]
    f32 = jnp.float32
    qf = q.astype(f32) * ATTN_SCALE
    qg = qf.reshape(B, Q, KV_HEADS, GROUP, HEAD_DIM)
    s = jnp.einsum('bqkgd,blkd->bqkgl', qg, k_own.astype(f32)).reshape(B, Q, N_HEADS, L)
    s = jnp.where(own_mask[None, :, None, :], s, -jnp.inf)
    m = s.max(-1, keepdims=True)
    p = jnp.exp(s - m)
    l = p.sum(-1, keepdims=True)
    acc = jnp.einsum('bqkgl,blkd->bqkgd', p.reshape(B, Q, KV_HEADS, GROUP, L), v_own.astype(f32)).reshape(B, Q, N_HEADS, HEAD_DIM)
    if n_slots > 0:
        nb = bmeans.shape[1]
        bs = jnp.einsum('bqkgd,bnkd->bqkgn', qg, bmeans.astype(f32)).reshape(B, Q, N_HEADS, nb)
        bs = jnp.where(jnp.arange(nb) < n_full, bs, -jnp.inf)
        _, top_i = lax.top_k(bs, n_slots)
        for slot in range(n_slots):
            k_sel, v_sel = fetch(top_i[..., slot])
            sc = jnp.einsum('bqhd,bqhld->bqhl', qf, k_sel.astype(f32))
            sc = jnp.where(slot < n_full, sc, -jnp.inf)
            m_new = jnp.maximum(m, sc.max(-1, keepdims=True))
            corr = jnp.exp(m - m_new)
            p = jnp.exp(sc - m_new)
            l = l * corr + p.sum(-1, keepdims=True)
            acc = acc * corr + jnp.einsum('bqhl,bqhld->bqhd', p, v_sel.astype(f32))
            m = m_new
    return (acc / l).astype(q.dtype)


def _moba_prompt(q, k, v):
    B, T = q.shape[:2]
    nb = -(-T // MOBA_BLOCK)
    pad = nb * MOBA_BLOCK - T
    k_p = jnp.pad(k, ((0, 0), (0, pad), (0, 0), (0, 0)))
    v_p = jnp.pad(v, ((0, 0), (0, pad), (0, 0), (0, 0)))
    kb = k_p.reshape(B, nb, MOBA_BLOCK, KV_HEADS, HEAD_DIM)
    vb = v_p.reshape(B, nb, MOBA_BLOCK, KV_HEADS, HEAD_DIM)
    bmeans = kb.astype(jnp.float32).mean(axis=2)
    kbt = kb.transpose(0, 1, 3, 2, 4)
    vbt = vb.transpose(0, 1, 3, 2, 4)
    n_slots = min(MOBA_TOPK, nb - 1)
    bidx = jnp.arange(B)[:, None, None]
    kvh = (jnp.arange(N_HEADS) // GROUP)[None, None, :]

    def fetch(idx):
        return kbt[bidx, idx, kvh], vbt[bidx, idx, kvh]

    nq = T // Q_BLOCK
    qb = q.reshape(B, nq, Q_BLOCK, N_HEADS, HEAD_DIM).transpose(1, 0, 2, 3, 4)

    def one_block(args):
        j, qj = args
        b = (j * Q_BLOCK) // MOBA_BLOCK
        k_own = lax.dynamic_slice_in_dim(k_p, b * MOBA_BLOCK, MOBA_BLOCK, axis=1)
        v_own = lax.dynamic_slice_in_dim(v_p, b * MOBA_BLOCK, MOBA_BLOCK, axis=1)
        q_pos = j * Q_BLOCK + jnp.arange(Q_BLOCK)
        k_pos = b * MOBA_BLOCK + jnp.arange(MOBA_BLOCK)
        mask = k_pos[None, :] <= q_pos[:, None]
        return _moba_core(qj, k_own, v_own, mask, bmeans, b, n_slots, fetch)

    out = lax.map(one_block, (jnp.arange(nq), qb))
    return out.transpose(1, 0, 2, 3, 4).reshape(B, T, N_HEADS, HEAD_DIM)


def _moba_sample(q, k, v, cache_k, cache_v, page_table, layer):
    DB, S = q.shape[:2]
    bp = MOBA_BLOCK // PAGE_SIZE
    n_pages = page_table.shape[1]
    n_full = (n_pages * PAGE_SIZE) // MOBA_BLOCK
    rem_pages = n_pages - n_full * bp
    own_pages = page_table[:, n_full * bp:]
    k_rem = cache_k[own_pages, :, layer].reshape(DB, rem_pages * PAGE_SIZE, KV_HEADS, HEAD_DIM)
    v_rem = cache_v[own_pages, :, layer].reshape(DB, rem_pages * PAGE_SIZE, KV_HEADS, HEAD_DIM)
    k_own = jnp.concatenate([k_rem.astype(k.dtype), k], axis=1)
    v_own = jnp.concatenate([v_rem.astype(v.dtype), v], axis=1)
    own_mask = jnp.concatenate([jnp.ones((S, rem_pages * PAGE_SIZE), bool), jnp.tril(jnp.ones((S, S), bool))], axis=1)
    n_slots = min(MOBA_TOPK, n_full)
    bmeans = None
    if n_slots > 0:
        page_means = cache_k[:, :, layer].astype(jnp.float32).mean(axis=1)
        bmeans = page_means[page_table[:, :n_full * bp]].reshape(DB, n_full, bp, KV_HEADS, HEAD_DIM).mean(axis=2)
    seq_idx = jnp.arange(DB)[:, None, None, None]
    kvh = (jnp.arange(N_HEADS) // GROUP)[None, None, :, None]

    def fetch(idx):
        phys = page_table[seq_idx, idx[..., None] * bp + jnp.arange(bp)]
        ks = cache_k[phys, :, layer, kvh].reshape(DB, S, N_HEADS, MOBA_BLOCK, HEAD_DIM)
        vs = cache_v[phys, :, layer, kvh].reshape(DB, S, N_HEADS, MOBA_BLOCK, HEAD_DIM)
        return ks, vs

    return _moba_core(q, k_own, v_own, own_mask, bmeans, n_full, n_slots, fetch)


def _mixer(h, i, conv0, h0, attend, prm):
    B, T, _ = h.shape
    proj = h @ prm['w_in'][i]
    xr, gr, q, k, v, gu, gv, gl = jnp.split(proj, SPLIT_POINTS, axis=-1)
    y_lru, h_last, conv_new = _rglru_branch(xr, gr, conv0, h0, prm['w_conv'][i], prm['b_conv'][i], prm['w_rg_a'][i], prm['b_rg_a'][i], prm['w_rg_x'][i], prm['b_rg_x'][i], prm['rg_lambda'][i])
    q = q.reshape(B, T, N_HEADS, HEAD_DIM)
    k = k.reshape(B, T, KV_HEADS, HEAD_DIM)
    v = v.reshape(B, T, KV_HEADS, HEAD_DIM)
    y_att = attend(i, q, k, v).reshape(B, T, N_HEADS * HEAD_DIM)
    y_gm, v_rows = _gmlp_branch(gu, gv, prm['gmlp_ln_g'][i], prm['gmlp_ln_b'][i], prm['w_spatial'][i], prm['b_spatial'][i])
    g_lru, g_att, g_gm = jnp.split(jax.nn.sigmoid(gl), N_BRANCH, axis=-1)
    merged = g_lru * (y_lru @ prm['w_br_lru'][i]) + g_att * (y_att @ prm['w_br_attn'][i]) + g_gm * (y_gm @ prm['w_br_gmlp'][i])
    return merged @ prm['w_o'][i], k, v, h_last, conv_new, v_rows


def _run_trunk(x, p, conv0, h0, attend, prm):
    ks, vs, hs, convs, gvs = [], [], [], [], []
    for i in range(DEPTH):
        f1 = _swiglu(x, prm['w_ffn_gate'][i, 0], prm['w_ffn_up'][i, 0], prm['w_ffn_down'][i, 0])
        x = _layer_norm(ALPHA * x + 0.5 * f1, prm['ln_g'][i, 0], prm['ln_b'][i, 0])
        y, k, v, h_last, conv_new, v_rows = _mixer(x, i, conv0[i], h0[i], attend, prm)
        x = _layer_norm(ALPHA * x + y, prm['ln_g'][i, 1], prm['ln_b'][i, 1])
        f2 = _swiglu(x, prm['w_ffn_gate'][i, 1], prm['w_ffn_up'][i, 1], prm['w_ffn_down'][i, 1])
        x = _layer_norm(ALPHA * x + 0.5 * f2, prm['ln_g'][i, 2], prm['ln_b'][i, 2])
        x = x + (p[i] @ prm['w_ple'][i]) * jax.nn.sigmoid(x @ prm['w_ple_gate'][i])
        ks.append(k); vs.append(v); hs.append(h_last); convs.append(conv_new); gvs.append(v_rows)
    return x, ks, vs, hs, convs, gvs


def setup_inputs(seed: int = 0) -> dict:
    key = jax.random.key(seed)
    keys = iter(jax.random.split(key, 48))
    f32 = jnp.float32

    def nrm(shape, scale):
        return jax.random.normal(next(keys), shape, f32) * scale

    n_pages = PAST_LEN // PAGE_SIZE
    n_used = DEC_BATCH * n_pages
    n_pool = n_used + max(1, n_used // 4)
    page_table = jax.random.permutation(next(keys), n_pool)[:n_used].reshape(DEC_BATCH, n_pages).astype(jnp.int32)
    a0 = jax.random.uniform(next(keys), (DEPTH, LRU_WIDTH), f32, 0.9, 0.999)
    s0 = a0 ** (1.0 / LRU_C)
    rg_lambda = jnp.log(s0) - jnp.log1p(-s0)
    return {
        'x_prompt': nrm((BATCH, SEQ, D_MODEL), 1.0),
        'x_sample': nrm((DEC_BATCH, DEC_SEQ, D_MODEL), 1.0),
        'cache_k': nrm((n_pool, PAGE_SIZE, DEPTH, KV_HEADS, HEAD_DIM), 1.0),
        'cache_v': nrm((n_pool, PAGE_SIZE, DEPTH, KV_HEADS, HEAD_DIM), 1.0),
        'state_rglru_h': nrm((DEPTH, DEC_BATCH, LRU_WIDTH), 0.5),
        'state_conv': nrm((DEPTH, DEC_BATCH, CONV_WIDTH - 1, LRU_WIDTH), 1.0),
        'page_table': page_table,
        'p_prompt': nrm((DEPTH, BATCH, SEQ, PLE_DIM), 1.0),
        'p_sample': nrm((DEPTH, DEC_BATCH, DEC_SEQ, PLE_DIM), 1.0),
        'w_in': nrm((DEPTH, D_MODEL, D_IN), D_MODEL ** -0.5),
        'w_conv': nrm((DEPTH, CONV_WIDTH, LRU_WIDTH), CONV_WIDTH ** -0.5),
        'b_conv': nrm((DEPTH, LRU_WIDTH), 0.01),
        'w_rg_a': nrm((DEPTH, LRU_HEADS, LRU_BLOCK, LRU_BLOCK), LRU_BLOCK ** -0.5),
        'b_rg_a': nrm((DEPTH, LRU_WIDTH), 0.01),
        'w_rg_x': nrm((DEPTH, LRU_HEADS, LRU_BLOCK, LRU_BLOCK), LRU_BLOCK ** -0.5),
        'b_rg_x': nrm((DEPTH, LRU_WIDTH), 0.01),
        'rg_lambda': rg_lambda,
        'gmlp_ln_g': 1.0 + nrm((DEPTH, GMLP_WIDTH), 0.02),
        'gmlp_ln_b': nrm((DEPTH, GMLP_WIDTH), 0.01),
        'w_spatial': nrm((DEPTH, GMLP_GROUPS, CHUNK, CHUNK), CHUNK ** -0.5),
        'b_spatial': 1.0 + nrm((DEPTH, GMLP_GROUPS, CHUNK), 0.01),
        'w_br_lru': nrm((DEPTH, LRU_WIDTH, D_MODEL), LRU_WIDTH ** -0.5),
        'w_br_attn': nrm((DEPTH, N_HEADS * HEAD_DIM, D_MODEL), (N_HEADS * HEAD_DIM) ** -0.5),
        'w_br_gmlp': nrm((DEPTH, GMLP_WIDTH, D_MODEL), GMLP_WIDTH ** -0.5),
        'w_o': nrm((DEPTH, D_MODEL, D_MODEL), BETA * D_MODEL ** -0.5),
        'w_ffn_gate': nrm((DEPTH, 2, D_MODEL, D_FF), D_MODEL ** -0.5),
        'w_ffn_up': nrm((DEPTH, 2, D_MODEL, D_FF), D_MODEL ** -0.5),
        'w_ffn_down': nrm((DEPTH, 2, D_FF, D_MODEL), BETA * D_FF ** -0.5),
        'ln_g': 1.0 + nrm((DEPTH, 3, D_MODEL), 0.02),
        'ln_b': nrm((DEPTH, 3, D_MODEL), 0.01),
        'w_ple': nrm((DEPTH, PLE_DIM, D_MODEL), PLE_DIM ** -0.5),
        'w_ple_gate': nrm((DEPTH, D_MODEL, D_MODEL), D_MODEL ** -0.5),
    }


def reference(x_prompt, x_sample, cache_k, cache_v, state_rglru_h, state_conv, page_table, p_prompt, p_sample, w_in, w_conv, b_conv, w_rg_a, b_rg_a, w_rg_x, b_rg_x, rg_lambda, gmlp_ln_g, gmlp_ln_b, w_spatial, b_spatial, w_br_lru, w_br_attn, w_br_gmlp, w_o, w_ffn_gate, w_ffn_up, w_ffn_down, ln_g, ln_b, w_ple, w_ple_gate):
    prm = dict(w_in=w_in, w_conv=w_conv, b_conv=b_conv, w_rg_a=w_rg_a, b_rg_a=b_rg_a, w_rg_x=w_rg_x, b_rg_x=b_rg_x, rg_lambda=rg_lambda, gmlp_ln_g=gmlp_ln_g, gmlp_ln_b=gmlp_ln_b, w_spatial=w_spatial, b_spatial=b_spatial, w_br_lru=w_br_lru, w_br_attn=w_br_attn, w_br_gmlp=w_br_gmlp, w_o=w_o, w_ffn_gate=w_ffn_gate, w_ffn_up=w_ffn_up, w_ffn_down=w_ffn_down, ln_g=ln_g, ln_b=ln_b, w_ple=w_ple, w_ple_gate=w_ple_gate)

    def attend_prompt(layer, q, k, v):
        return _moba_prompt(q, k, v)

    def attend_sample(layer, q, k, v):
        return _moba_sample(q, k, v, cache_k, cache_v, page_table, layer)

    bsz = x_prompt.shape[0]
    conv0 = jnp.zeros((DEPTH, bsz, CONV_WIDTH - 1, LRU_WIDTH), x_prompt.dtype)
    h0 = jnp.zeros((DEPTH, bsz, LRU_WIDTH), x_prompt.dtype)
    y_prompt, kp, vp, hp, cp, _ = _run_trunk(x_prompt, p_prompt, conv0, h0, attend_prompt, prm)
    y_sample, ks, vs, hs, cs, gvs = _run_trunk(x_sample, p_sample, state_conv, state_rglru_h, attend_sample, prm)
    return (y_prompt, y_sample, jnp.stack(kp, axis=2), jnp.stack(vp, axis=2), jnp.stack(hp), jnp.stack(cp), jnp.stack(ks, axis=2), jnp.stack(vs, axis=2), jnp.stack(hs), jnp.stack(cs), jnp.stack(gvs))
```

```python
import functools

import jax
import jax.numpy as jnp
from jax import lax
from jax.experimental import pallas as pl
from jax.experimental.pallas import tpu as pltpu

D_MODEL = 1024
DEPTH = 2
PAGE_SIZE = 128
PLE_DIM = 256
LRU_WIDTH = D_MODEL
LRU_HEADS = 8
LRU_BLOCK = LRU_WIDTH // LRU_HEADS
CONV_WIDTH = 4
LRU_C = 8.0
N_HEADS = 8
KV_HEADS = 4
HEAD_DIM = D_MODEL // N_HEADS
GROUP = N_HEADS // KV_HEADS
MOBA_BLOCK = 256
MOBA_TOPK = 3
Q_BLOCK = 128
ATTN_SCALE = HEAD_DIM ** -0.5
GMLP_WIDTH = D_MODEL
GMLP_GROUPS = 8
GMLP_GW = GMLP_WIDTH // GMLP_GROUPS
CHUNK = 128
D_FF = 2816
N_BRANCH = 3
ALPHA = (2.0 * DEPTH) ** 0.25
LN_EPS = 1e-5
IN_SIZES = (LRU_WIDTH, LRU_WIDTH, N_HEADS * HEAD_DIM, KV_HEADS * HEAD_DIM, KV_HEADS * HEAD_DIM,
            GMLP_WIDTH, GMLP_WIDTH, N_BRANCH * D_MODEL)
D_IN = sum(IN_SIZES)
OFF_XR, OFF_GR, OFF_Q, OFF_K, OFF_V, OFF_GU, OFF_GV, OFF_GL = (sum(IN_SIZES[:j]) for j in range(len(IN_SIZES)))
KV_W = KV_HEADS * HEAD_DIM

V7X_LANES = 128
V7X_SUBLANES = 8
V7X_MXU_DIM = 256
V7X_VMEM_LIMIT = 56 * 1024 * 1024

F32 = jnp.float32
BF16 = jnp.bfloat16
NEG_INF = float("-inf")


def _cparams(sem):
    return pltpu.CompilerParams(dimension_semantics=sem, vmem_limit_bytes=V7X_VMEM_LIMIT)


def _dot(a, b):
    return jnp.dot(a, b, preferred_element_type=F32)


def _dot_nt(a, b, precision=None):
    return lax.dot_general(a, b, (((1,), (1,)), ((), ())), preferred_element_type=F32, precision=precision)


def _layer_norm(y, g, b):
    mu = jnp.mean(y, axis=-1, keepdims=True)
    yc = y - mu
    var = jnp.mean(yc * yc, axis=-1, keepdims=True)
    return yc * lax.rsqrt(var + LN_EPS) * g + b


def _gelu(x):
    return jax.nn.gelu(x)


def _resident(shape):
    nd = len(shape)
    return pl.BlockSpec(shape, lambda *_: (0,) * nd, pipeline_mode=pl.Buffered(1))


FFN_CHUNK = V7X_MXU_DIM


def _ffn_kernel(*refs, with_ple):
    if with_ple:
        x_ref, wg_ref, wu_ref, wd_ref, g_ref, b_ref, p_ref, wple_ref, wpg_ref, o_ref = refs
    else:
        x_ref, wg_ref, wu_ref, wd_ref, g_ref, b_ref, o_ref = refs
    x = x_ref[...]
    xb = x.astype(BF16)
    acc = jnp.zeros(x.shape, F32)
    for c in range(D_FF // FFN_CHUNK):
        sl = slice(c * FFN_CHUNK, (c + 1) * FFN_CHUNK)
        gate = _dot(xb, wg_ref[:, sl])
        up = _dot(xb, wu_ref[:, sl])
        act = (gate * jax.nn.sigmoid(gate) * up).astype(BF16)
        acc = acc + _dot(act, wd_ref[sl, :])
    xn = _layer_norm(ALPHA * x + 0.5 * acc, g_ref[...], b_ref[...])
    if with_ple:
        emb = _dot(p_ref[...].astype(BF16), wple_ref[...])
        gate = jax.nn.sigmoid(_dot(xn.astype(BF16), wpg_ref[...]))
        xn = xn + emb * gate
    o_ref[...] = xn


def ffn_ln(x, wg, wu, wd, g, b, ple=None):
    n = x.shape[0]
    tm = min(512, n)
    row = lambda i: (i, 0)
    in_specs = [pl.BlockSpec((tm, D_MODEL), row), _resident(wg.shape), _resident(wu.shape), _resident(wd.shape),
                _resident(g.shape), _resident(b.shape)]
    args = [x, wg, wu, wd, g, b]
    if ple is not None:
        p, wple, wpg = ple
        in_specs += [pl.BlockSpec((tm, PLE_DIM), row), _resident(wple.shape), _resident(wpg.shape)]
        args += [p, wple, wpg]
    return pl.pallas_call(
        functools.partial(_ffn_kernel, with_ple=ple is not None),
        grid=(n // tm,),
        in_specs=in_specs,
        out_specs=pl.BlockSpec((tm, D_MODEL), row),
        out_shape=jax.ShapeDtypeStruct((n, D_MODEL), F32),
        compiler_params=_cparams(("parallel",)),
        name="ffn_ln_ple" if ple is not None else "ffn_ln",
    )(*args)


def _in_proj_kernel(x_ref, w_ref, o_ref):
    o_ref[...] = _dot(x_ref[...].astype(BF16), w_ref[...])


def in_proj(x, w):
    n = x.shape[0]
    tm = min(1024, n)
    tn = 1024
    return pl.pallas_call(
        _in_proj_kernel,
        grid=(n // tm, D_IN // tn),
        in_specs=[pl.BlockSpec((tm, D_MODEL), lambda i, j: (i, 0)), pl.BlockSpec((D_MODEL, tn), lambda i, j: (0, j))],
        out_specs=pl.BlockSpec((tm, tn), lambda i, j: (i, j)),
        out_shape=jax.ShapeDtypeStruct((n, D_IN), F32),
        compiler_params=_cparams(("parallel", "parallel")),
        name="in_proj",
    )(x, w)


CONV_PAD = V7X_SUBLANES


def _rglru_kernel(xr_ref, gr_ref, conv0_ref, h0_ref, wconv_ref, bconv_ref, wa_ref, ba_ref, wx_ref, bx_ref, lam_ref,
                  y_ref, hlast_ref, xbuf, a_buf, b_buf, h_buf, *, tt):
    c = pl.program_id(1)
    tail = CONV_WIDTH - 1

    @pl.when(c == 0)
    def _():
        xbuf[CONV_PAD - tail:CONV_PAD, :] = conv0_ref[...]
        h_buf[...] = h0_ref[...]

    xbuf[CONV_PAD:CONV_PAD + tt, :] = xr_ref[...]
    wc = wconv_ref[...]
    xc = bconv_ref[...] + xbuf[CONV_PAD:CONV_PAD + tt, :] * wc[tail:tail + 1, :]
    for j in range(tail):
        xc = xc + xbuf[CONV_PAD - tail + j:CONV_PAD - tail + j + tt, :] * wc[j:j + 1, :]
    xbuf[CONV_PAD - tail:CONV_PAD, :] = xbuf[CONV_PAD + tt - tail:CONV_PAD + tt, :]

    lam = lam_ref[...]
    neg = -lam
    softplus = jnp.maximum(neg, 0.0) + jnp.log1p(jnp.exp(-jnp.abs(neg)))
    xcb = xc.astype(BF16)
    for h in range(LRU_HEADS):
        sl = slice(h * LRU_BLOCK, (h + 1) * LRU_BLOCK)
        r = jax.nn.sigmoid(_dot(xcb[:, sl], wa_ref[h]) + ba_ref[:, sl])
        ig = jax.nn.sigmoid(_dot(xcb[:, sl], wx_ref[h]) + bx_ref[:, sl])
        log_a = -LRU_C * r * softplus[:, sl]
        a = jnp.exp(log_a)
        mult = jnp.sqrt(-jnp.tanh(log_a) * (a * a + 1.0))
        a_buf[:, sl] = a
        b_buf[:, sl] = mult * (ig * xc[:, sl])

    def step(i, h):
        for u in range(V7X_SUBLANES):
            t = i * V7X_SUBLANES + u
            h = a_buf[pl.ds(t, 1), :] * h + b_buf[pl.ds(t, 1), :]
            b_buf[pl.ds(t, 1), :] = h
        return h

    h = lax.fori_loop(0, tt // V7X_SUBLANES, step, h_buf[...])
    h_buf[...] = h
    hlast_ref[...] = h
    y_ref[...] = (b_buf[...] * _gelu(gr_ref[...])).astype(y_ref.dtype)


def rglru(proj, conv0, h0, wconv, bconv, wa, ba, wx, bx, lam, *, batch, seq):
    tt = min(256, seq)
    nt = seq // tt
    tail = CONV_WIDTH - 1
    vec = lambda b, c: (0, 0)
    y, hlast = pl.pallas_call(
        functools.partial(_rglru_kernel, tt=tt),
        grid=(batch, nt),
        in_specs=[
            pl.BlockSpec((tt, LRU_WIDTH), lambda b, c: (b * nt + c, OFF_XR // LRU_WIDTH)),
            pl.BlockSpec((tt, LRU_WIDTH), lambda b, c: (b * nt + c, OFF_GR // LRU_WIDTH)),
            pl.BlockSpec((None, tail, LRU_WIDTH), lambda b, c: (b, 0, 0)),
            pl.BlockSpec((None, 1, LRU_WIDTH), lambda b, c: (b, 0, 0)),
            pl.BlockSpec((CONV_WIDTH, LRU_WIDTH), vec),
            pl.BlockSpec((1, LRU_WIDTH), vec),
            pl.BlockSpec((LRU_HEADS, LRU_BLOCK, LRU_BLOCK), lambda b, c: (0, 0, 0)),
            pl.BlockSpec((1, LRU_WIDTH), vec),
            pl.BlockSpec((LRU_HEADS, LRU_BLOCK, LRU_BLOCK), lambda b, c: (0, 0, 0)),
            pl.BlockSpec((1, LRU_WIDTH), vec),
            pl.BlockSpec((1, LRU_WIDTH), vec),
        ],
        out_specs=[
            pl.BlockSpec((tt, LRU_WIDTH), lambda b, c: (b * nt + c, 0)),
            pl.BlockSpec((None, 1, LRU_WIDTH), lambda b, c: (b, 0, 0)),
        ],
        out_shape=[jax.ShapeDtypeStruct((batch * seq, LRU_WIDTH), BF16),
                   jax.ShapeDtypeStruct((batch, 1, LRU_WIDTH), F32)],
        scratch_shapes=[pltpu.VMEM((CONV_PAD + tt, LRU_WIDTH), F32), pltpu.VMEM((tt, LRU_WIDTH), F32),
                        pltpu.VMEM((tt, LRU_WIDTH), F32), pltpu.VMEM((1, LRU_WIDTH), F32)],
        compiler_params=_cparams(("parallel", "arbitrary")),
        name="rglru",
    )(proj, proj, conv0, h0.reshape(batch, 1, LRU_WIDTH), wconv, bconv, wa, ba, wx, bx, lam)
    return y, hlast.reshape(batch, LRU_WIDTH)


def _gmlp_kernel(gu_ref, gv_ref, lng_ref, lnb_ref, w_ref, bs_ref, *out_refs, chunk_len, emit_v):
    o_ref = out_refs[0]
    tg = gu_ref.shape[0]
    v = _layer_norm(_gelu(gv_ref[...]), lng_ref[...], lnb_ref[...])
    if emit_v:
        out_refs[1][...] = v
    vb = v.astype(BF16)
    row = lax.broadcasted_iota(jnp.int32, (CHUNK, CHUNK), 0)
    col = lax.broadcasted_iota(jnp.int32, (CHUNK, CHUNK), 1)
    keep = (col <= row) & ((row // chunk_len) == (col // chunk_len))
    for g in range(GMLP_GROUPS):
        cols = slice(g * GMLP_GW, (g + 1) * GMLP_GW)
        wm = jnp.where(keep, w_ref[g], 0.0).astype(BF16)
        bias = bs_ref[g]
        for ci in range(tg // CHUNK):
            rows = slice(ci * CHUNK, (ci + 1) * CHUNK)
            mix = _dot(wm, vb[rows, cols]) + bias
            o_ref[rows, cols] = (_gelu(gu_ref[rows, cols]) * mix).astype(o_ref.dtype)


def gmlp(proj, lng, lnb, w_s, b_s, *, seq, emit_v):
    n = proj.shape[0]
    chunk_len = min(seq, CHUNK)
    rep = CHUNK // chunk_len
    w_eff = jnp.tile(w_s[:, :chunk_len, :chunk_len], (1, rep, rep))
    b_eff = jnp.tile(b_s[:, :chunk_len], (1, rep))[:, :, None]
    tg = min(256, n)
    out_shape = [jax.ShapeDtypeStruct((n, GMLP_WIDTH), BF16)]
    out_specs = [pl.BlockSpec((tg, GMLP_WIDTH), lambda i: (i, 0))]
    if emit_v:
        out_shape.append(jax.ShapeDtypeStruct((n, GMLP_WIDTH), F32))
        out_specs.append(pl.BlockSpec((tg, GMLP_WIDTH), lambda i: (i, 0)))
    outs = pl.pallas_call(
        functools.partial(_gmlp_kernel, chunk_len=chunk_len, emit_v=emit_v),
        grid=(n // tg,),
        in_specs=[
            pl.BlockSpec((tg, GMLP_WIDTH), lambda i: (i, OFF_GU // GMLP_WIDTH)),
            pl.BlockSpec((tg, GMLP_WIDTH), lambda i: (i, OFF_GV // GMLP_WIDTH)),
            pl.BlockSpec((1, GMLP_WIDTH), lambda i: (0, 0)),
            pl.BlockSpec((1, GMLP_WIDTH), lambda i: (0, 0)),
            pl.BlockSpec((GMLP_GROUPS, CHUNK, CHUNK), lambda i: (0, 0, 0)),
            pl.BlockSpec((GMLP_GROUPS, CHUNK, 1), lambda i: (0, 0, 0)),
        ],
        out_specs=out_specs,
        out_shape=out_shape,
        compiler_params=_cparams(("parallel",)),
        name="gmlp",
    )(proj, proj, lng, lnb, w_eff, b_eff)
    return (outs[0], outs[1]) if emit_v else (outs[0], None)


def _select_topk(scores, n_valid):
    lane = lax.broadcasted_iota(jnp.int32, scores.shape, 1)
    sc = jnp.where(lane < n_valid, scores, NEG_INF)
    sel = jnp.zeros(scores.shape, F32)
    for _ in range(MOBA_TOPK):
        mx = jnp.max(sc, axis=1, keepdims=True)
        first = jnp.min(jnp.where(sc == mx, lane, V7X_LANES), axis=1, keepdims=True)
        pick = (lane == first) & (mx > NEG_INF)
        sel = jnp.where(pick, 1.0, sel)
        sc = jnp.where(pick, NEG_INF, sc)
    return sel


def _online_update(s, vb, m, l, acc):
    m_new = jnp.maximum(m, jnp.max(s, axis=1, keepdims=True))
    corr = jnp.exp(m - m_new)
    p = jnp.exp(s - m_new)
    l = l * corr + jnp.sum(p, axis=1, keepdims=True)
    acc = acc * corr + _dot(p.astype(BF16), vb)
    return m_new, l, acc


def _moba_prompt_kernel(q_ref, k_ref, v_ref, o_ref, kb, vb, bm, *, nb):
    j = pl.program_id(2)
    rows = GROUP * Q_BLOCK

    @pl.when(j == 0)
    def _():
        kb[...] = k_ref[...].astype(BF16)
        vb[...] = v_ref[...].astype(BF16)
        bm[...] = jnp.zeros(bm.shape, F32)
        for n in range(nb):
            blk = k_ref[n * MOBA_BLOCK:(n + 1) * MOBA_BLOCK, :]
            bm[n:n + 1, :] = jnp.sum(blk, axis=0, keepdims=True) * (1.0 / MOBA_BLOCK)

    own = (j * Q_BLOCK) // MOBA_BLOCK
    qs = q_ref[...] * ATTN_SCALE
    q2 = jnp.concatenate([qs[:, g * HEAD_DIM:(g + 1) * HEAD_DIM] for g in range(GROUP)], axis=0)
    q2b = q2.astype(BF16)

    sel = _select_topk(_dot_nt(q2, bm[...], precision=lax.Precision.HIGHEST), own)
    lane = lax.broadcasted_iota(jnp.int32, sel.shape, 1)

    start = pl.multiple_of(own * MOBA_BLOCK, MOBA_BLOCK)
    s = _dot_nt(q2b, kb[pl.ds(start, MOBA_BLOCK), :])
    q_pos = j * Q_BLOCK + lax.broadcasted_iota(jnp.int32, s.shape, 0) % Q_BLOCK
    k_pos = own * MOBA_BLOCK + lax.broadcasted_iota(jnp.int32, s.shape, 1)
    s = jnp.where(k_pos <= q_pos, s, NEG_INF)
    m = jnp.max(s, axis=1, keepdims=True)
    p = jnp.exp(s - m)
    l = jnp.sum(p, axis=1, keepdims=True)
    acc = _dot(p.astype(BF16), vb[pl.ds(start, MOBA_BLOCK), :])

    def body(n, carry):
        m, l, acc = carry
        picked = jnp.sum(jnp.where(lane == n, sel, 0.0), axis=1, keepdims=True)
        st = pl.multiple_of(n * MOBA_BLOCK, MOBA_BLOCK)
        s = _dot_nt(q2b, kb[pl.ds(st, MOBA_BLOCK), :])
        s = jnp.where(picked > 0.5, s, NEG_INF)
        return _online_update(s, vb[pl.ds(st, MOBA_BLOCK), :], m, l, acc)

    m, l, acc = lax.fori_loop(0, own, body, (m, l, acc))
    out = acc / l
    for g in range(GROUP):
        o_ref[:, g * HEAD_DIM:(g + 1) * HEAD_DIM] = out[g * Q_BLOCK:(g + 1) * Q_BLOCK, :].astype(o_ref.dtype)


def moba_prompt(proj, *, batch, seq):
    assert seq % MOBA_BLOCK == 0 and seq // MOBA_BLOCK <= V7X_LANES
    nb = seq // MOBA_BLOCK
    nq = seq // Q_BLOCK
    gw = GROUP * HEAD_DIM
    return pl.pallas_call(
        functools.partial(_moba_prompt_kernel, nb=nb),
        grid=(batch, KV_HEADS, nq),
        in_specs=[
            pl.BlockSpec((Q_BLOCK, gw), lambda b, k, j: (b * nq + j, OFF_Q // gw + k)),
            pl.BlockSpec((seq, HEAD_DIM), lambda b, k, j: (b, OFF_K // HEAD_DIM + k)),
            pl.BlockSpec((seq, HEAD_DIM), lambda b, k, j: (b, OFF_V // HEAD_DIM + k)),
        ],
        out_specs=pl.BlockSpec((Q_BLOCK, gw), lambda b, k, j: (b * nq + j, k)),
        out_shape=jax.ShapeDtypeStruct((batch * seq, N_HEADS * HEAD_DIM), BF16),
        scratch_shapes=[pltpu.VMEM((seq, HEAD_DIM), BF16), pltpu.VMEM((seq, HEAD_DIM), BF16),
                        pltpu.VMEM((V7X_LANES, HEAD_DIM), F32)],
        compiler_params=_cparams(("parallel", "parallel", "arbitrary")),
        name="moba_prompt",
    )(proj, proj, proj)


PAGE_ROWS = PAGE_SIZE * DEPTH * KV_HEADS
ROW_STRIDE = DEPTH * KV_HEADS
PAGES_PER_BLOCK = MOBA_BLOCK // PAGE_SIZE
MEAN_PAGES = 8


def _block_means_kernel(pt_ref, *refs):
    page_refs, o_ref = refs[:-1], refs[-1]
    for blk in range(MEAN_PAGES // PAGES_PER_BLOCK):
        tot = jnp.zeros((ROW_STRIDE, HEAD_DIM), F32)
        for pg in range(PAGES_PER_BLOCK):
            x = page_refs[blk * PAGES_PER_BLOCK + pg][...]
            tot = tot + jnp.sum(x.reshape(PAGE_SIZE, ROW_STRIDE, HEAD_DIM), axis=0)
        o_ref[blk] = tot * (1.0 / MOBA_BLOCK)


def block_means(cache3, page_table):
    nseq, n_pages = page_table.shape
    nblk = n_pages // PAGES_PER_BLOCK
    per = MEAN_PAGES // PAGES_PER_BLOCK
    in_specs = [pl.BlockSpec((None, PAGE_ROWS, HEAD_DIM),
                             functools.partial(lambda s, c, pt, i: (pt[s, c * MEAN_PAGES + i], 0, 0), i=i))
                for i in range(MEAN_PAGES)]
    return pl.pallas_call(
        _block_means_kernel,
        grid_spec=pltpu.PrefetchScalarGridSpec(
            num_scalar_prefetch=1,
            grid=(nseq, n_pages // MEAN_PAGES),
            in_specs=in_specs,
            out_specs=pl.BlockSpec((None, per, ROW_STRIDE, HEAD_DIM), lambda s, c, pt: (s, c, 0, 0)),
        ),
        out_shape=jax.ShapeDtypeStruct((nseq, nblk, ROW_STRIDE, HEAD_DIM), F32),
        compiler_params=_cparams(("parallel", "parallel")),
        name="block_means",
    )(page_table, *([cache3] * MEAN_PAGES))


def _moba_sample_kernel(pt_ref, q_ref, kn_ref, vn_ref, bm_ref, k0_ref, k1_ref, v0_ref, v1_ref, o_ref,
                        qs, sel, m_s, l_s, acc_s, *, layer, nblk, dec_seq):
    n = pl.program_id(1)
    rows = GROUP * dec_seq

    @pl.when(n == 0)
    def _():
        q = q_ref[...] * ATTN_SCALE
        for kv in range(KV_HEADS):
            sub = layer * KV_HEADS + kv
            qk = jnp.concatenate(
                [q[:, (kv * GROUP + g) * HEAD_DIM:(kv * GROUP + g + 1) * HEAD_DIM] for g in range(GROUP)], axis=0)
            qs[kv] = qk
            means = bm_ref[pl.ds(sub, nblk, stride=ROW_STRIDE), :]
            means = jnp.concatenate([means, jnp.zeros((V7X_LANES - nblk, HEAD_DIM), F32)], axis=0)
            sel[kv] = _select_topk(_dot_nt(qk, means, precision=lax.Precision.HIGHEST), nblk)
            kn = kn_ref[:, kv * HEAD_DIM:(kv + 1) * HEAD_DIM]
            vn = vn_ref[:, kv * HEAD_DIM:(kv + 1) * HEAD_DIM]
            q_idx = lax.broadcasted_iota(jnp.int32, (rows, 1), 0) % dec_seq
            s_cols = []
            for t in range(dec_seq):
                st = jnp.sum(qk * kn[t:t + 1, :], axis=1, keepdims=True)
                s_cols.append(jnp.where(q_idx >= t, st, NEG_INF))
            m = s_cols[0]
            for t in range(1, dec_seq):
                m = jnp.maximum(m, s_cols[t])
            l = jnp.zeros((rows, 1), F32)
            acc = jnp.zeros((rows, HEAD_DIM), F32)
            for t in range(dec_seq):
                p = jnp.exp(s_cols[t] - m)
                l = l + p
                acc = acc + p * vn[t:t + 1, :]
            m_s[kv] = m
            l_s[kv] = l
            acc_s[kv] = acc

    for kv in range(KV_HEADS):
        sub = layer * KV_HEADS + kv
        sk = sel[kv]
        lane = lax.broadcasted_iota(jnp.int32, sk.shape, 1)
        picked = jnp.sum(jnp.where(lane == n, sk, 0.0), axis=1, keepdims=True)
        kblk = jnp.concatenate([k0_ref[pl.ds(sub, PAGE_SIZE, stride=ROW_STRIDE), :],
                                k1_ref[pl.ds(sub, PAGE_SIZE, stride=ROW_STRIDE), :]], axis=0).astype(BF16)
        vblk = jnp.concatenate([v0_ref[pl.ds(sub, PAGE_SIZE, stride=ROW_STRIDE), :],
                                v1_ref[pl.ds(sub, PAGE_SIZE, stride=ROW_STRIDE), :]], axis=0).astype(BF16)
        s = _dot_nt(qs[kv].astype(BF16), kblk)
        s = jnp.where(picked > 0.5, s, NEG_INF)
        m, l, acc = _online_update(s, vblk, m_s[kv], l_s[kv], acc_s[kv])
        m_s[kv] = m
        l_s[kv] = l
        acc_s[kv] = acc

    @pl.when(n == nblk - 1)
    def _():
        for kv in range(KV_HEADS):
            out = acc_s[kv] / l_s[kv]
            for g in range(GROUP):
                h = kv * GROUP + g
                o_ref[:, h * HEAD_DIM:(h + 1) * HEAD_DIM] = out[g * dec_seq:(g + 1) * dec_seq, :].astype(o_ref.dtype)


def moba_sample(proj, bmeans, cache_k3, cache_v3, page_table, *, layer, nseq, dec_seq):
    n_pages = page_table.shape[1]
    assert n_pages % PAGES_PER_BLOCK == 0, "cached pages must fill whole key blocks"
    assert PAGES_PER_BLOCK == 2
    nblk = n_pages // PAGES_PER_BLOCK
    assert 0 < nblk <= V7X_LANES
    rows = GROUP * dec_seq
    qw = N_HEADS * HEAD_DIM
    page = lambda i: pl.BlockSpec((None, PAGE_ROWS, HEAD_DIM),
                                  lambda s, n, pt: (pt[s, n * PAGES_PER_BLOCK + i], 0, 0))
    return pl.pallas_call(
        functools.partial(_moba_sample_kernel, layer=layer, nblk=nblk, dec_seq=dec_seq),
        grid_spec=pltpu.PrefetchScalarGridSpec(
            num_scalar_prefetch=1,
            grid=(nseq, nblk),
            in_specs=[
                pl.BlockSpec((dec_seq, qw), lambda s, n, pt: (s, OFF_Q // qw)),
                pl.BlockSpec((dec_seq, KV_W), lambda s, n, pt: (s, OFF_K // KV_W)),
                pl.BlockSpec((dec_seq, KV_W), lambda s, n, pt: (s, OFF_V // KV_W)),
                pl.BlockSpec((None, nblk * ROW_STRIDE, HEAD_DIM), lambda s, n, pt: (s, 0, 0)),
                page(0), page(1), page(0), page(1),
            ],
            out_specs=pl.BlockSpec((dec_seq, qw), lambda s, n, pt: (s, 0)),
            scratch_shapes=[pltpu.VMEM((KV_HEADS, rows, HEAD_DIM), F32), pltpu.VMEM((KV_HEADS, rows, V7X_LANES), F32),
                            pltpu.VMEM((KV_HEADS, rows, 1), F32), pltpu.VMEM((KV_HEADS, rows, 1), F32),
                            pltpu.VMEM((KV_HEADS, rows, HEAD_DIM), F32)],
        ),
        out_shape=jax.ShapeDtypeStruct((nseq * dec_seq, qw), BF16),
        compiler_params=_cparams(("parallel", "arbitrary")),
        name="moba_sample",
    )(page_table, proj, proj, proj, bmeans.reshape(nseq, nblk * ROW_STRIDE, HEAD_DIM),
      cache_k3, cache_k3, cache_v3, cache_v3)


def _merge_kernel(x_ref, gl_ref, yl_ref, ya_ref, yg_ref, wl_ref, wa_ref, wg_ref, wo_ref, g_ref, b_ref, o_ref):
    merged = None
    for bi, (y_ref, w_ref) in enumerate(((yl_ref, wl_ref), (ya_ref, wa_ref), (yg_ref, wg_ref))):
        gate = jax.nn.sigmoid(gl_ref[:, bi * D_MODEL:(bi + 1) * D_MODEL])
        term = gate * _dot(y_ref[...], w_ref[...])
        merged = term if merged is None else merged + term
    y = _dot(merged.astype(BF16), wo_ref[...])
    o_ref[...] = _layer_norm(ALPHA * x_ref[...] + y, g_ref[...], b_ref[...])


def merge_ln(x, proj, y_lru, y_att, y_gm, wl, wa, wg, wo, g, b):
    n = x.shape[0]
    tm = min(256, n)
    row = lambda i: (i, 0)
    glw = N_BRANCH * D_MODEL
    return pl.pallas_call(
        _merge_kernel,
        grid=(n // tm,),
        in_specs=[pl.BlockSpec((tm, D_MODEL), row), pl.BlockSpec((tm, glw), lambda i: (i, OFF_GL // glw)),
                  pl.BlockSpec((tm, D_MODEL), row), pl.BlockSpec((tm, D_MODEL), row), pl.BlockSpec((tm, D_MODEL), row),
                  _resident(wl.shape), _resident(wa.shape), _resident(wg.shape), _resident(wo.shape),
                  _resident(g.shape), _resident(b.shape)],
        out_specs=pl.BlockSpec((tm, D_MODEL), row),
        out_shape=jax.ShapeDtypeStruct((n, D_MODEL), F32),
        compiler_params=_cparams(("parallel",)),
        name="merge_ln",
    )(x, proj, y_lru, y_att, y_gm, wl, wa, wg, wo, g, b)


def _run_trunk(x, p, conv0, h0, attend, w, *, batch, seq, emit_v):
    ks, vs, hs, convs, gvs = [], [], [], [], []
    tail = CONV_WIDTH - 1
    for i in range(DEPTH):
        row = lambda a: a.reshape(1, -1)
        x = ffn_ln(x, w['ffn_gate'][i, 0], w['ffn_up'][i, 0], w['ffn_down'][i, 0],
                   row(w['ln_g'][i, 0]), row(w['ln_b'][i, 0]))
        proj = in_proj(x, w['w_in'][i])
        y_lru, h_last = rglru(proj, conv0[i], h0[i], w['w_conv'][i], row(w['b_conv'][i]), w['w_rg_a'][i],
                              row(w['b_rg_a'][i]), w['w_rg_x'][i], row(w['b_rg_x'][i]), row(w['rg_lambda'][i]),
                              batch=batch, seq=seq)
        y_att = attend(i, proj)
        y_gm, v_rows = gmlp(proj, row(w['gmlp_ln_g'][i]), row(w['gmlp_ln_b'][i]), w['w_spatial'][i],
                            w['b_spatial'][i], seq=seq, emit_v=emit_v)
        x = merge_ln(x, proj, y_lru, y_att, y_gm, w['w_br_lru'][i], w['w_br_attn'][i], w['w_br_gmlp'][i],
                     w['w_o'][i], row(w['ln_g'][i, 1]), row(w['ln_b'][i, 1]))
        x = ffn_ln(x, w['ffn_gate'][i, 1], w['ffn_up'][i, 1], w['ffn_down'][i, 1],
                   row(w['ln_g'][i, 2]), row(w['ln_b'][i, 2]), ple=(p[i], w['w_ple'][i], w['w_ple_gate'][i]))
        ks.append(proj[:, OFF_K:OFF_K + KV_W].reshape(batch, seq, KV_HEADS, HEAD_DIM))
        vs.append(proj[:, OFF_V:OFF_V + KV_W].reshape(batch, seq, KV_HEADS, HEAD_DIM))
        hs.append(h_last)
        xr = proj[:, OFF_XR:OFF_XR + LRU_WIDTH].reshape(batch, seq, LRU_WIDTH)
        convs.append(jnp.concatenate([conv0[i], xr[:, max(seq - tail, 0):]], axis=1)[:, -tail:])
        gvs.append(v_rows)
    return x, ks, vs, hs, convs, gvs


def kernel(x_prompt, x_sample, cache_k, cache_v, state_rglru_h, state_conv, page_table, p_prompt, p_sample, w_in, w_conv, b_conv, w_rg_a, b_rg_a, w_rg_x, b_rg_x, rg_lambda, gmlp_ln_g, gmlp_ln_b, w_spatial, b_spatial, w_br_lru, w_br_attn, w_br_gmlp, w_o, w_ffn_gate, w_ffn_up, w_ffn_down, ln_g, ln_b, w_ple, w_ple_gate):
    bsz, seq, _ = x_prompt.shape
    nseq, dec_seq, _ = x_sample.shape
    n_pool = cache_k.shape[0]
    assert cache_k.shape[1:] == (PAGE_SIZE, DEPTH, KV_HEADS, HEAD_DIM)
    w = dict(
        w_in=w_in.astype(BF16), w_conv=w_conv, b_conv=b_conv, w_rg_a=w_rg_a.astype(BF16), b_rg_a=b_rg_a,
        w_rg_x=w_rg_x.astype(BF16), b_rg_x=b_rg_x, rg_lambda=rg_lambda, gmlp_ln_g=gmlp_ln_g, gmlp_ln_b=gmlp_ln_b,
        w_spatial=w_spatial, b_spatial=b_spatial, w_br_lru=w_br_lru.astype(BF16), w_br_attn=w_br_attn.astype(BF16),
        w_br_gmlp=w_br_gmlp.astype(BF16), w_o=w_o.astype(BF16), ffn_gate=w_ffn_gate.astype(BF16),
        ffn_up=w_ffn_up.astype(BF16), ffn_down=w_ffn_down.astype(BF16), ln_g=ln_g, ln_b=ln_b,
        w_ple=w_ple.astype(BF16), w_ple_gate=w_ple_gate.astype(BF16))

    def attend_prompt(layer, proj):
        return moba_prompt(proj, batch=bsz, seq=seq)

    cache_k3 = cache_k.reshape(n_pool, PAGE_ROWS, HEAD_DIM)
    cache_v3 = cache_v.reshape(n_pool, PAGE_ROWS, HEAD_DIM)
    bmeans = block_means(cache_k3, page_table)

    def attend_sample(layer, proj):
        return moba_sample(proj, bmeans, cache_k3, cache_v3, page_table, layer=layer, nseq=nseq, dec_seq=dec_seq)

    conv0 = jnp.zeros((DEPTH, bsz, CONV_WIDTH - 1, LRU_WIDTH), x_prompt.dtype)
    h0 = jnp.zeros((DEPTH, bsz, LRU_WIDTH), x_prompt.dtype)
    y_p, kp, vp, hp, cp, _ = _run_trunk(
        x_prompt.reshape(bsz * seq, D_MODEL), p_prompt.reshape(DEPTH, bsz * seq, PLE_DIM), conv0, h0,
        attend_prompt, w, batch=bsz, seq=seq, emit_v=False)
    y_s, ks, vs, hs, cs, gvs = _run_trunk(
        x_sample.reshape(nseq * dec_seq, D_MODEL), p_sample.reshape(DEPTH, nseq * dec_seq, PLE_DIM), state_conv,
        state_rglru_h, attend_sample, w, batch=nseq, seq=dec_seq, emit_v=True)
    gv = jnp.stack(gvs).reshape(DEPTH, nseq, dec_seq, GMLP_WIDTH)
    return (y_p.reshape(bsz, seq, D_MODEL), y_s.reshape(nseq, dec_seq, D_MODEL),
            jnp.stack(kp, axis=2), jnp.stack(vp, axis=2), jnp.stack(hp), jnp.stack(cp),
            jnp.stack(ks, axis=2), jnp.stack(vs, axis=2), jnp.stack(hs), jnp.stack(cs), gv)
```
